```python
import jax
import jax.numpy as jnp
from jax import lax
import numpy as np

D_MODEL = 1024
BATCH = 1
SEQ = 16384
DEPTH = 1
DEC_BATCH = 128
DEC_SEQ = 8
PAST_LEN = 16384
PAGE_SIZE = 128

N_META = 16
EPS = 1e-6
A_HEADS = 4
A_DK = 128
A_DV = 128
A_CHUNK = 128
B_HEADS = 4
Q_LORA = 256
KV_LORA = 128
D_NOPE = 128
D_ROPE = 64
D_V = 128
ROPE_THETA = 10000.0
Q_BLOCK = 128
MLA_SCALE = (D_NOPE + D_ROPE) ** -0.5
MIX_WIDTH = A_HEADS * A_DV + B_HEADS * D_V
IN_SPLITS = (A_HEADS * A_DK, 2 * A_HEADS * A_DK, 2 * A_HEADS * A_DK + A_HEADS * A_DV, 2 * A_HEADS * A_DK + 2 * A_HEADS * A_DV, 2 * A_HEADS * A_DK + 2 * A_HEADS * A_DV + Q_LORA, 2 * A_HEADS * A_DK + 2 * A_HEADS * A_DV + Q_LORA + KV_LORA)
IN_COLS = IN_SPLITS[-1] + D_ROPE
PK_HEADS = 8
N_KEYS = 128
N_EXPERTS = N_KEYS * N_KEYS
PK_DIM = 128
PK_TOPK = 16
PEER_BLOCK = 256

kernel_name = 'hymba_hgrn2_mla_peer_step'


def rmsnorm(x, w):
    xf = x.astype(jnp.float32)
    y = xf * lax.rsqrt(jnp.mean(xf * xf, axis=-1, keepdims=True) + EPS)
    return (y * w.astype(jnp.float32)).astype(x.dtype)


def rope(x, pos):
    half = D_ROPE // 2
    inv = ROPE_THETA ** (-jnp.arange(half, dtype=jnp.float32) * 2.0 / D_ROPE)
    ang = pos.astype(jnp.float32)[:, None] * inv[None, :]
    shape = (pos.shape[0],) + (1,) * (x.ndim - 3) + (half,)
    cos = jnp.cos(ang).reshape(shape)
    sin = jnp.sin(ang).reshape(shape)
    x1 = x[..., :half].astype(jnp.float32)
    x2 = x[..., half:].astype(jnp.float32)
    return jnp.concatenate([x1 * cos - x2 * sin, x2 * cos + x1 * sin], axis=-1).astype(x.dtype)


def project(h, pos, lb, w_in, q_norm_w, w_uq, kv_norm_w, w_uk):
    B, L, _ = h.shape
    z = h @ w_in
    q_a, f_a, i_a, g_a, c_q, c_kv, k_r = jnp.split(z, IN_SPLITS, axis=-1)
    f = lb + (1.0 - lb) * jax.nn.sigmoid(f_a.astype(jnp.float32))
    k_a = (1.0 - f).reshape(B, L, A_HEADS, A_DK)
    g_log = jnp.log(f).reshape(B, L, A_HEADS, A_DK)
    q_a = q_a.reshape(B, L, A_HEADS, A_DK)
    v_a = i_a.reshape(B, L, A_HEADS, A_DV)
    gate_a = g_a.reshape(B, L, A_HEADS, A_DV)
    q_b = (rmsnorm(c_q, q_norm_w) @ w_uq).reshape(B, L, B_HEADS, D_NOPE + D_ROPE)
    q_lat = jnp.einsum('blhn,chn->blhc', q_b[..., :D_NOPE], w_uk)
    q_pe = rope(q_b[..., D_NOPE:], pos)
    c = rmsnorm(c_kv, kv_norm_w)
    k_pe = rope(k_r, pos)
    return (q_a, k_a, v_a, g_log, gate_a), (q_lat, q_pe, c, k_pe)


def gla_chunk(S, q, k, v, g):
    q, k, v, g = (t.astype(jnp.float32) for t in (q, k, v, g))
    T = q.shape[1]
    b = jnp.cumsum(g, axis=1)
    causal = jnp.tril(jnp.ones((T, T), dtype=bool))[None, :, :, None, None]
    decay = jnp.exp(jnp.where(causal, b[:, :, None] - b[:, None, :], -jnp.inf))
    scores = jnp.einsum('bthk,btshk,bshk->bhts', q, decay, k)
    o = jnp.einsum('bthk,bhkv->bthv', q * jnp.exp(b), S) + jnp.einsum('bhts,bshv->bthv', scores, v)
    b_end = b[:, -1]
    S_new = jnp.exp(b_end)[..., None] * S + jnp.einsum('bshk,bshv->bhkv', k * jnp.exp(b_end[:, None] - b), v)
    return o, S_new


def hgrn_prompt(q, k, v, g):
    B = q.shape[0]
    S0 = jnp.zeros((B, A_HEADS, A_DK, A_DV), jnp.float32)
    o_meta, S = gla_chunk(S0, q[:, :N_META], k[:, :N_META], v[:, :N_META], g[:, :N_META])

    def chunks(t):
        t = t[:, N_META:]
        return t.reshape((B, t.shape[1] // A_CHUNK, A_CHUNK) + t.shape[2:]).swapaxes(0, 1)

    def step(S, xs):
        o, S = gla_chunk(S, *xs)
        return S, o

    S, o = lax.scan(step, S, (chunks(q), chunks(k), chunks(v), chunks(g)))
    o = o.swapaxes(0, 1).reshape((B, -1) + o.shape[3:])
    return jnp.concatenate([o_meta, o], axis=1), S


def mla_prompt(q_lat, q_pe, c, k_pe):
    B, L = c.shape[:2]
    nb = -(-L // Q_BLOCK)
    Lp = nb * Q_BLOCK

    def pad(t):
        return jnp.pad(t, [(0, 0), (0, Lp - L)] + [(0, 0)] * (t.ndim - 2))

    q_lat, q_pe, c, k_pe = pad(q_lat), pad(q_pe), pad(c), pad(k_pe)
    key_pos = jnp.arange(Lp)

    def block(i):
        start = i * Q_BLOCK
        ql = lax.dynamic_slice_in_dim(q_lat, start, Q_BLOCK, axis=1)
        qr = lax.dynamic_slice_in_dim(q_pe, start, Q_BLOCK, axis=1)
        s = jnp.einsum('bqhc,bkc->bhqk', ql, c) + jnp.einsum('bqhr,bkr->bhqk', qr, k_pe)
        s = s.astype(jnp.float32) * MLA_SCALE
        mask = key_pos[None, :] <= (start + jnp.arange(Q_BLOCK))[:, None]
        p = jax.nn.softmax(jnp.where(mask, s, -jnp.inf), axis=-1).astype(c.dtype)
        return jnp.einsum('bhqk,bkc->bqhc', p, c)

    out = lax.map(block, jnp.arange(nb))
    out = out.swapaxes(0, 1).reshape(B, Lp, B_HEADS, KV_LORA)
    return out[:, :L]


def mla_sample(q_lat, q_pe, c_new, kpe_new, c_past, kpe_past):
    T = c_new.shape[1]
    P = c_past.shape[1]
    s_past = jnp.einsum('bqhc,bkc->bhqk', q_lat, c_past) + jnp.einsum('bqhr,bkr->bhqk', q_pe, kpe_past)
    s_new = jnp.einsum('bqhc,bkc->bhqk', q_lat, c_new) + jnp.einsum('bqhr,bkr->bhqk', q_pe, kpe_new)
    causal = jnp.tril(jnp.ones((T, T), dtype=bool))
    s_new = jnp.where(causal, s_new.astype(jnp.float32) * MLA_SCALE, -jnp.inf)
    s = jnp.concatenate([s_past.astype(jnp.float32) * MLA_SCALE, s_new], axis=-1)
    p = jax.nn.softmax(s, axis=-1).astype(c_new.dtype)
    return jnp.einsum('bhqk,bkc->bqhc', p[..., :P], c_past) + jnp.einsum('bhqk,bkc->bqhc', p[..., P:], c_new)


def merge(o_a, gate_a, lat, w_uv, a_out_norm_w, b_out_norm_w, w_out):
    B, L = o_a.shape[:2]
    ya = rmsnorm(o_a, a_out_norm_w) * jax.nn.silu(gate_a)
    yb = rmsnorm(jnp.einsum('blhc,chv->blhv', lat, w_uv), b_out_norm_w)
    y = jnp.concatenate([ya.reshape(B, L, -1), yb.reshape(B, L, -1)], axis=-1)
    return y @ w_out


def peer_ffn(h, w_pq, sub_keys, expert_u, expert_v):
    B, L, D = h.shape
    T = B * L
    nb = -(-T // PEER_BLOCK)
    xf = jnp.pad(h.reshape(T, D), ((0, nb * PEER_BLOCK - T), (0, 0))).reshape(nb, PEER_BLOCK, D)

    def block(xb):
        q = (xb @ w_pq).reshape(PEER_BLOCK, 2, PK_HEADS, PK_DIM)
        s = jnp.einsum('tphd,phnd->tphn', q, sub_keys).astype(jnp.float32)
        s_top, i_top = lax.top_k(s, PK_TOPK)
        cand = (s_top[:, 0, :, :, None] + s_top[:, 1, :, None, :]).reshape(PEER_BLOCK, PK_HEADS, PK_TOPK * PK_TOPK)
        cand_idx = (i_top[:, 0, :, :, None] * N_KEYS + i_top[:, 1, :, None, :]).reshape(PEER_BLOCK, PK_HEADS, PK_TOPK * PK_TOPK)
        best, pos = lax.top_k(cand, PK_TOPK)
        eidx = jnp.take_along_axis(cand_idx, pos, axis=-1)
        gate = jax.nn.softmax(best, axis=-1).astype(xb.dtype)
        act = jax.nn.gelu(jnp.einsum('td,thkd->thk', xb, expert_u[eidx]), approximate=False)
        return jnp.einsum('thk,thkd->td', gate * act, expert_v[eidx])

    y = lax.map(block, xf)
    return y.reshape(nb * PEER_BLOCK, D)[:T].reshape(B, L, D)


def setup_inputs(seed: int = 0) -> dict:
    key = jax.random.key(seed)
    ks = jax.random.split(key, 26)
    n_pages = PAST_LEN // PAGE_SIZE
    n_used = DEC_BATCH * n_pages
    n_pool = n_used + n_used // 4
    nrm = lambda k, shape, s=1.0: jax.random.normal(k, shape, jnp.float32) * s
    gain = lambda k, shape: 1.0 + 0.02 * jax.random.normal(k, shape, jnp.float32)
    page_table = jax.random.permutation(ks[5], n_pool)[:n_used].reshape(DEC_BATCH, n_pages).astype(jnp.int32)
    return {
        'x_prompt': nrm(ks[0], (BATCH, SEQ, D_MODEL)),
        'x_sample': nrm(ks[1], (DEC_BATCH, DEC_SEQ, D_MODEL)),
        'cache_kv_latent': nrm(ks[2], (DEPTH, n_pool, PAGE_SIZE, KV_LORA)),
        'cache_k_rope': nrm(ks[3], (DEPTH, n_pool, PAGE_SIZE, D_ROPE)),
        'state_hgrn': nrm(ks[4], (DEPTH, DEC_BATCH, A_HEADS, A_DK, A_DV), 0.3),
        'page_table': page_table,
        'meta_tokens': nrm(ks[6], (N_META, D_MODEL)),
        'attn_norm_w': gain(ks[7], (DEPTH, D_MODEL)),
        'w_in': nrm(ks[8], (DEPTH, D_MODEL, IN_COLS), D_MODEL ** -0.5),
        'lb_logits': nrm(ks[9], (DEPTH + 1, A_HEADS * A_DK), 0.5),
        'q_norm_w': gain(ks[10], (DEPTH, Q_LORA)),
        'w_uq': nrm(ks[11], (DEPTH, Q_LORA, B_HEADS * (D_NOPE + D_ROPE)), Q_LORA ** -0.5),
        'kv_norm_w': gain(ks[12], (DEPTH, KV_LORA)),
        'w_uk': nrm(ks[13], (DEPTH, KV_LORA, B_HEADS, D_NOPE), KV_LORA ** -0.5),
        'w_uv': nrm(ks[14], (DEPTH, KV_LORA, B_HEADS, D_V), KV_LORA ** -0.5),
        'a_out_norm_w': gain(ks[15], (DEPTH, A_HEADS, A_DV)),
        'b_out_norm_w': gain(ks[16], (DEPTH, B_HEADS, D_V)),
        'w_out': nrm(ks[17], (DEPTH, MIX_WIDTH, D_MODEL), MIX_WIDTH ** -0.5),
        'ffn_norm_w': gain(ks[18], (DEPTH, D_MODEL)),
        'w_peer_q': nrm(ks[19], (DEPTH, D_MODEL, 2 * PK_HEADS * PK_DIM), D_MODEL ** -0.5),
        'peer_sub_keys': nrm(ks[20], (DEPTH, 2, PK_HEADS, N_KEYS, PK_DIM), PK_DIM ** -0.5),
        'peer_u': nrm(ks[21], (DEPTH, N_EXPERTS, D_MODEL), D_MODEL ** -0.5),
        'peer_v': nrm(ks[22], (DEPTH, N_EXPERTS, D_MODEL), 0.1),
        'final_norm_w': gain(ks[23], (D_MODEL,)),
    }


def reference(x_prompt, x_sample, cache_kv_latent, cache_k_rope, state_hgrn, page_table, meta_tokens, attn_norm_w, w_in, lb_logits, q_norm_w, w_uq, kv_norm_w, w_uk, w_uv, a_out_norm_w, b_out_norm_w, w_out, ffn_norm_w, w_peer_q, peer_sub_keys, peer_u, peer_v, final_norm_w):
    B = x_prompt.shape[0]
    Bd, T = x_sample.shape[:2]
    L = N_META + x_prompt.shape[1]
    hp = jnp.concatenate([jnp.broadcast_to(meta_tokens[None].astype(x_prompt.dtype), (B, N_META, D_MODEL)), x_prompt], axis=1)
    hs = x_sample
    pos_p = jnp.arange(L, dtype=jnp.int32)
    pos_s = PAST_LEN + jnp.arange(T, dtype=jnp.int32)
    lower_bounds = jnp.cumsum(jax.nn.softmax(lb_logits.astype(jnp.float32), axis=0), axis=0)
    lat_p, kpe_p, st_p, lat_s, kpe_s, st_s = [], [], [], [], [], []
    for l in range(DEPTH):
        lb = lower_bounds[l]
        (qa, ka, va, ga, gate_a), (q_lat, q_pe, c, k_pe) = project(rmsnorm(hp, attn_norm_w[l]), pos_p, lb, w_in[l], q_norm_w[l], w_uq[l], kv_norm_w[l], w_uk[l])
        o_a, S_p = hgrn_prompt(qa, ka, va, ga)
        lat = mla_prompt(q_lat, q_pe, c, k_pe)
        hp = hp + merge(o_a.astype(hp.dtype), gate_a, lat, w_uv[l], a_out_norm_w[l], b_out_norm_w[l], w_out[l])
        hp = hp + peer_ffn(rmsnorm(hp, ffn_norm_w[l]), w_peer_q[l], peer_sub_keys[l], peer_u[l], peer_v[l])
        lat_p.append(c.astype(cache_kv_latent.dtype))
        kpe_p.append(k_pe.astype(cache_k_rope.dtype))
        st_p.append(S_p.astype(state_hgrn.dtype))
        (qa, ka, va, ga, gate_a), (q_lat, q_pe, c, k_pe) = project(rmsnorm(hs, attn_norm_w[l]), pos_s, lb, w_in[l], q_norm_w[l], w_uq[l], kv_norm_w[l], w_uk[l])
        o_a, S_s = gla_chunk(state_hgrn[l].astype(jnp.float32), qa, ka, va, ga)
        c_past = cache_kv_latent[l][page_table].reshape(Bd, -1, KV_LORA)
        kpe_past = cache_k_rope[l][page_table].reshape(Bd, -1, D_ROPE)
        lat = mla_sample(q_lat, q_pe, c, k_pe, c_past.astype(c.dtype), kpe_past.astype(k_pe.dtype))
        hs = hs + merge(o_a.astype(hs.dtype), gate_a, lat, w_uv[l], a_out_norm_w[l], b_out_norm_w[l], w_out[l])
        hs = hs + peer_ffn(rmsnorm(hs, ffn_norm_w[l]), w_peer_q[l], peer_sub_keys[l], peer_u[l], peer_v[l])
        lat_s.append(c.astype(cache_kv_latent.dtype))
        kpe_s.append(k_pe.astype(cache_k_rope.dtype))
        st_s.append(S_s.astype(state_hgrn.dtype))
    y_prompt = rmsnorm(hp[:, N_META:], final_norm_w)
    y_sample = rmsnorm(hs, final_norm_w)
    return (y_prompt, y_sample, jnp.stack(lat_p), jnp.stack(kpe_p), jnp.stack(st_p), jnp.stack(lat_s), jnp.stack(kpe_s), jnp.stack(st_s))
```

```python
import functools
import math

import jax
import jax.numpy as jnp
from jax import lax
from jax.experimental import pallas as pl
from jax.experimental.pallas import tpu as pltpu

F32 = jnp.float32
BF16 = jnp.bfloat16

D_MODEL = 1024
N_META = 16
EPS = 1e-6
A_HEADS = 4
A_DK = 128
A_DV = 128
B_HEADS = 4
Q_LORA = 256
KV_LORA = 128
D_NOPE = 128
D_ROPE = 64
D_V = 128
ROPE_THETA = 10000.0
MLA_SCALE = (D_NOPE + D_ROPE) ** -0.5
PK_HEADS = 8
N_KEYS = 128
PK_DIM = 128
PK_TOPK = 16
PAGE_SIZE = 128

LANES = 128
A_WIDTH = A_HEADS * A_DK
IN_COLS = 4 * A_WIDTH + Q_LORA + KV_LORA + D_ROPE
IN_COLS_PAD = IN_COLS + (LANES - D_ROPE)
Q_CAT = 2 * LANES
SUB_CHUNK = 16
NEG_BIG = -1e30
VMEM_LIMIT = 56 * 1024 * 1024


def _cparams(sem):
    return pltpu.CompilerParams(dimension_semantics=sem, vmem_limit_bytes=VMEM_LIMIT)


def _rms(x, w):
    return x * lax.rsqrt(jnp.mean(x * x, axis=-1, keepdims=True) + EPS) * w


def _dot(a, b):
    return jnp.dot(a, b, preferred_element_type=F32)


def _dot_nt(a, b):
    return lax.dot_general(a, b, (((1,), (1,)), ((), ())), preferred_element_type=F32)


def _dot_tn(a, b):
    return lax.dot_general(a, b, (((0,), (0,)), ((), ())), preferred_element_type=F32)


def _proj_kernel(x_ref, cos_ref, sin_ref, anw_ref, win_ref, lb_ref, qnw_ref, wuq_ref, kvnw_ref, wuk_ref,
                 qa_ref, ka_ref, va_ref, g_ref, gate_ref, c_ref, kpe_ref, qcat_ref, kcat_ref):
    tm = x_ref.shape[0]
    xn = _rms(x_ref[...], anw_ref[...]).astype(BF16)
    z = _dot(xn, win_ref[...])
    w = A_WIDTH
    qa_ref[...] = z[:, 0:w]
    lb = lb_ref[...]
    f = lb + (1.0 - lb) * jax.nn.sigmoid(z[:, w:2 * w])
    ka_ref[...] = 1.0 - f
    g_ref[...] = jnp.log(f)
    va_ref[...] = z[:, 2 * w:3 * w]
    gate_ref[...] = z[:, 3 * w:4 * w]
    o = 4 * w
    cq = _rms(z[:, o:o + Q_LORA], qnw_ref[...]).astype(BF16)
    qb = _dot(cq, wuq_ref[...])
    o += Q_LORA
    c = _rms(z[:, o:o + KV_LORA], kvnw_ref[...])
    o += KV_LORA
    kr = z[:, o:o + LANES]

    cos = cos_ref[...]
    sin = sin_ref[...]
    lane = lax.broadcasted_iota(jnp.int32, (tm, LANES), 1)
    first_half = (lane % D_ROPE) < (D_ROPE // 2)
    low = lane < D_ROPE

    def rope(t):
        rot = jnp.where(first_half, pltpu.roll(t, LANES - D_ROPE // 2, 1), pltpu.roll(t, D_ROPE // 2, 1))
        return t * cos + rot * sin

    nope_w = B_HEADS * D_NOPE
    qr = [rope(qb[:, nope_w:nope_w + LANES]) * MLA_SCALE, rope(qb[:, nope_w + LANES:nope_w + 2 * LANES]) * MLA_SCALE]
    for h in range(B_HEADS):
        ql = _dot(qb[:, h * D_NOPE:(h + 1) * D_NOPE].astype(BF16), wuk_ref[h]) * MLA_SCALE
        qp = jnp.where(low if h % 2 == 0 else jnp.logical_not(low), qr[h // 2], 0.0)
        qcat_ref[:, h * Q_CAT:h * Q_CAT + LANES] = ql.astype(BF16)
        qcat_ref[:, h * Q_CAT + LANES:(h + 1) * Q_CAT] = qp.astype(BF16)

    kp = rope(kr)
    c_ref[...] = c
    kpe_ref[...] = kp[:, :D_ROPE]
    kcat_ref[:, :LANES] = c.astype(BF16)
    kcat_ref[:, LANES:] = (kp + pltpu.roll(kp, D_ROPE, 1)).astype(BF16)


def _proj_call(x, cos, sin, wts, tm):
    rows = x.shape[0]
    assert rows % tm == 0
    row = lambda w_: pl.BlockSpec((tm, w_), lambda i: (i, 0))
    full = lambda a: pl.BlockSpec(a.shape, lambda i: (0,) * a.ndim)
    out_shapes = (
        [jax.ShapeDtypeStruct((rows, A_WIDTH), F32)] * 5
        + [jax.ShapeDtypeStruct((rows, KV_LORA), F32), jax.ShapeDtypeStruct((rows, D_ROPE), F32),
           jax.ShapeDtypeStruct((rows, B_HEADS * Q_CAT), BF16), jax.ShapeDtypeStruct((rows, Q_CAT), BF16)])
    out_specs = [row(A_WIDTH)] * 5 + [row(KV_LORA), row(D_ROPE), row(B_HEADS * Q_CAT), row(Q_CAT)]
    names = ("anw", "win", "lb", "qnw", "wuq", "kvnw", "wuk")
    return pl.pallas_call(
        _proj_kernel,
        out_shape=out_shapes,
        grid=(rows // tm,),
        in_specs=[row(D_MODEL), row(LANES), row(LANES)] + [full(wts[n]) for n in names],
        out_specs=out_specs,
        compiler_params=_cparams(("parallel",)),
        name="proj",
    )(x, cos, sin, *[wts[n] for n in names])


def _cumsum_rows(g):
    n = g.shape[0]
    row = lax.broadcasted_iota(jnp.int32, g.shape, 0)
    b = g
    sh = 1
    while sh < n:
        b = b + jnp.where(row >= sh, pltpu.roll(b, sh, 0), 0.0)
        sh *= 2
    return b


def _hgrn_kernel(q_ref, k_ref, v_ref, g_ref, s0_ref, o_ref, sout_ref, s_scr, *, sub):
    i = pl.program_id(1)
    tc = q_ref.shape[0]
    nsub = tc // sub

    @pl.when(i == 0)
    def _():
        s_scr[...] = s0_ref[0]

    for h in range(A_HEADS):
        sl = slice(h * A_DK, (h + 1) * A_DK)
        q = q_ref[:, sl]
        k = k_ref[:, sl]
        v = v_ref[:, sl]
        g = g_ref[:, sl]
        b = _cumsum_rows(g)
        excl = b - g
        state = s_scr[h]
        o = _dot((q * jnp.exp(b)).astype(BF16), state.astype(BF16))
        bstart = jnp.concatenate(
            [jnp.broadcast_to(excl[j * sub:j * sub + 1, :], (sub, A_DK)) for j in range(nsub)], axis=0)
        qt = (q * jnp.exp(b - bstart)).astype(BF16)
        vb = v.astype(BF16)
        parts = []
        for j in range(nsub):
            nk = (j + 1) * sub
            kt = (k[:nk] * jnp.exp(excl[j * sub:j * sub + 1, :] - b[:nk])).astype(BF16)
            sc = _dot_nt(qt[j * sub:(j + 1) * sub], kt)
            r = lax.broadcasted_iota(jnp.int32, sc.shape, 0) + j * sub
            cidx = lax.broadcasted_iota(jnp.int32, sc.shape, 1)
            sc = jnp.where(cidx <= r, sc, 0.0)
            parts.append(_dot(sc.astype(BF16), vb[:nk]))
        o_ref[:, sl] = o + jnp.concatenate(parts, axis=0)
        bend = b[tc - 1:tc, :]
        khat = (k * jnp.exp(bend - b)).astype(BF16)
        decay_col = jnp.transpose(jnp.broadcast_to(jnp.exp(bend), (A_DK, A_DK)))
        s_scr[h] = decay_col * state + _dot(jnp.transpose(khat), vb)

    @pl.when(i == pl.num_programs(1) - 1)
    def _():
        sout_ref[0] = s_scr[...]


def _hgrn_call(q, k, v, g, s0, tc, sub):
    nseq = s0.shape[0]
    rows = q.shape[0]
    nchunk = rows // (nseq * tc)
    blk = pl.BlockSpec((tc, A_WIDTH), lambda b, i: (b * nchunk + i, 0))
    st = pl.BlockSpec((1, A_HEADS, A_DK, A_DV), lambda b, i: (b, 0, 0, 0))
    return pl.pallas_call(
        functools.partial(_hgrn_kernel, sub=sub),
        out_shape=[jax.ShapeDtypeStruct((rows, A_WIDTH), F32), jax.ShapeDtypeStruct(s0.shape, F32)],
        grid=(nseq, nchunk),
        in_specs=[blk, blk, blk, blk, st],
        out_specs=[blk, st],
        scratch_shapes=[pltpu.VMEM((A_HEADS, A_DK, A_DV), F32)],
        compiler_params=_cparams(("parallel", "arbitrary")),
        name="hgrn",
    )(q, k, v, g, s0)


def _mla_prompt_kernel(qi_ref, ki_ref, q_ref, k_ref, km_ref, o_ref, q4_scr, m_scr, l_scr, acc_scr, *, n_meta_pad):
    s = pl.program_id(0)
    qi = qi_ref[s]
    ki = ki_ref[s]
    tq = q_ref.shape[0]
    tk = k_ref.shape[0]

    def update(sc, vals, first):
        m_prev = m_scr[...]
        m_new = jnp.max(sc, axis=-1, keepdims=True)
        if not first:
            m_new = jnp.maximum(m_prev, m_new)
        p = jnp.exp(sc - m_new)
        pv = _dot(p.astype(BF16), vals)
        if first:
            l_scr[...] = jnp.sum(p, axis=-1, keepdims=True)
            acc_scr[...] = pv
        else:
            alpha = jnp.exp(m_prev - m_new)
            l_scr[...] = alpha * l_scr[...] + jnp.sum(p, axis=-1, keepdims=True)
            acc_scr[...] = alpha * acc_scr[...] + pv
        m_scr[...] = m_new

    @pl.when(ki == 0)
    def _():
        for h in range(B_HEADS):
            q4_scr[h * tq:(h + 1) * tq, :] = q_ref[:, h * Q_CAT:(h + 1) * Q_CAT]
        km = km_ref[...]
        sc = _dot_nt(q4_scr[...], km)
        col = lax.broadcasted_iota(jnp.int32, sc.shape, 1)
        update(jnp.where(col >= n_meta_pad, sc, NEG_BIG), km[:, :KV_LORA], True)

    kc = k_ref[...]
    sc = _dot_nt(q4_scr[...], kc)

    @pl.when(ki < qi)
    def _():
        update(sc, kc[:, :KV_LORA], False)

    @pl.when(ki == qi)
    def _():
        row = lax.broadcasted_iota(jnp.int32, sc.shape, 0) % tq
        col = lax.broadcasted_iota(jnp.int32, sc.shape, 1)
        update(jnp.where(col <= row, sc, NEG_BIG), kc[:, :KV_LORA], False)
        out = acc_scr[...] / l_scr[...]
        for h in range(B_HEADS):
            o_ref[:, h * KV_LORA:(h + 1) * KV_LORA] = out[h * tq:(h + 1) * tq]


def _mla_prompt_call(qcat, kcat, kcat_meta, t):
    rows = qcat.shape[0]
    nq = rows // t
    pairs = [(a, b) for a in range(nq) for b in range(a + 1)]
    qi = jnp.asarray([p[0] for p in pairs], jnp.int32)
    ki = jnp.asarray([p[1] for p in pairs], jnp.int32)
    grid_spec = pltpu.PrefetchScalarGridSpec(
        num_scalar_prefetch=2,
        grid=(len(pairs),),
        in_specs=[pl.BlockSpec((t, B_HEADS * Q_CAT), lambda s, qi, ki: (qi[s], 0)),
                  pl.BlockSpec((t, Q_CAT), lambda s, qi, ki: (ki[s], 0)),
                  pl.BlockSpec(kcat_meta.shape, lambda s, qi, ki: (0, 0))],
        out_specs=pl.BlockSpec((t, B_HEADS * KV_LORA), lambda s, qi, ki: (qi[s], 0)),
        scratch_shapes=[pltpu.VMEM((B_HEADS * t, Q_CAT), BF16), pltpu.VMEM((B_HEADS * t, 1), F32),
                        pltpu.VMEM((B_HEADS * t, 1), F32), pltpu.VMEM((B_HEADS * t, KV_LORA), F32)])
    return pl.pallas_call(
        functools.partial(_mla_prompt_kernel, n_meta_pad=kcat_meta.shape[0] - N_META),
        out_shape=jax.ShapeDtypeStruct((rows, B_HEADS * KV_LORA), F32),
        grid_spec=grid_spec,
        compiler_params=_cparams(("arbitrary",)),
        name="mla_prompt",
    )(qi, ki, qcat, kcat, kcat_meta)


def _mla_sample_kernel(pt_ref, q_ref, cn_ref, kn_ref, ckv_hbm, ckr_hbm, o_ref, cbuf, kbuf, sem, *, n_pages):
    b = pl.program_id(0)
    nb = pl.num_programs(0)
    slot = b % 2

    def page_copies(bb, sl, j):
        page = pt_ref[bb * n_pages + j]
        return (pltpu.make_async_copy(ckv_hbm.at[page], cbuf.at[sl, j], sem.at[0, sl]),
                pltpu.make_async_copy(ckr_hbm.at[page], kbuf.at[sl, j], sem.at[1, sl]))

    def start_all(bb, sl):
        def body(j, carry):
            for cp in page_copies(bb, sl, j):
                cp.start()
            return carry
        lax.fori_loop(0, n_pages, body, 0)

    @pl.when(b == 0)
    def _():
        start_all(0, 0)

    @pl.when(b + 1 < nb)
    def _():
        start_all(b + 1, 1 - slot)

    def wait_body(j, carry):
        for cp in page_copies(b, slot, j):
            cp.wait()
        return carry
    lax.fori_loop(0, n_pages, wait_body, 0)

    t = q_ref.shape[1]
    q = q_ref[0]
    ql = jnp.concatenate([q[:, h * Q_CAT:h * Q_CAT + LANES] for h in range(B_HEADS)], axis=0)
    qp = []
    for h in range(B_HEADS):
        tile = q[:, h * Q_CAT + LANES:(h + 1) * Q_CAT]
        if h % 2 == 1:
            tile = pltpu.roll(tile, D_ROPE, 1)
        qp.append(tile[:, :D_ROPE])
    qp = jnp.concatenate(qp, axis=0)
    ql = ql.astype(BF16)
    qp = qp.astype(BF16)

    npast = n_pages * PAGE_SIZE
    cpast = cbuf[slot].reshape(npast, KV_LORA).astype(BF16)
    kpast = kbuf[slot].reshape(npast, D_ROPE).astype(BF16)
    s_past = _dot_nt(ql, cpast) + _dot_nt(qp, kpast)
    cnew = cn_ref[0].astype(BF16)
    knew = kn_ref[0].astype(BF16)
    s_new = _dot_nt(ql, cnew) + _dot_nt(qp, knew)
    row = lax.broadcasted_iota(jnp.int32, s_new.shape, 0) % t
    col = lax.broadcasted_iota(jnp.int32, s_new.shape, 1)
    s_new = jnp.where(col <= row, s_new, NEG_BIG)
    m = jnp.maximum(jnp.max(s_past, axis=-1, keepdims=True), jnp.max(s_new, axis=-1, keepdims=True))
    p_past = jnp.exp(s_past - m)
    p_new = jnp.exp(s_new - m)
    l = jnp.sum(p_past, axis=-1, keepdims=True) + jnp.sum(p_new, axis=-1, keepdims=True)
    out = (_dot(p_past.astype(BF16), cpast) + _dot(p_new.astype(BF16), cnew)) / l
    for h in range(B_HEADS):
        o_ref[0, :, h * KV_LORA:(h + 1) * KV_LORA] = out[h * t:(h + 1) * t]


def _mla_sample_call(page_table, q3, c3, k3, cache_kv, cache_kr):
    nb, t, _ = q3.shape
    n_pages = page_table.shape[1]
    blk = lambda w_: pl.BlockSpec((1, t, w_), lambda b, pt: (b, 0, 0))
    grid_spec = pltpu.PrefetchScalarGridSpec(
        num_scalar_prefetch=1,
        grid=(nb,),
        in_specs=[blk(B_HEADS * Q_CAT), blk(KV_LORA), blk(D_ROPE),
                  pl.BlockSpec(memory_space=pl.ANY), pl.BlockSpec(memory_space=pl.ANY)],
        out_specs=blk(B_HEADS * KV_LORA),
        scratch_shapes=[pltpu.VMEM((2, n_pages, PAGE_SIZE, KV_LORA), F32),
                        pltpu.VMEM((2, n_pages, PAGE_SIZE, D_ROPE), F32),
                        pltpu.SemaphoreType.DMA((2, 2))])
    return pl.pallas_call(
        functools.partial(_mla_sample_kernel, n_pages=n_pages),
        out_shape=jax.ShapeDtypeStruct((nb, t, B_HEADS * KV_LORA), F32),
        grid_spec=grid_spec,
        compiler_params=_cparams(("arbitrary",)),
        name="mla_sample",
    )(page_table.reshape(-1), q3, c3, k3, cache_kv, cache_kr)


def _merge_kernel(x_ref, oa_ref, gate_ref, lat_ref, anw_ref, bnw_ref, wuv_ref, wout_ref, h_ref):
    ys = []
    for h in range(A_HEADS):
        sl = slice(h * A_DV, (h + 1) * A_DV)
        ys.append(_rms(oa_ref[:, sl], anw_ref[:, sl]) * jax.nn.silu(gate_ref[:, sl]))
    for h in range(B_HEADS):
        sl = slice(h * D_V, (h + 1) * D_V)
        ys.append(_rms(_dot(lat_ref[:, sl].astype(BF16), wuv_ref[h]), bnw_ref[:, sl]))
    y = jnp.concatenate(ys, axis=-1).astype(BF16)
    h_ref[...] = x_ref[...] + _dot(y, wout_ref[...])


def _merge_call(x, oa, gate, lat, wts, tm):
    rows = x.shape[0]
    row = lambda w_: pl.BlockSpec((tm, w_), lambda i: (i, 0))
    full = lambda a: pl.BlockSpec(a.shape, lambda i: (0,) * a.ndim)
    names = ("aonw", "bonw", "wuv", "wout")
    return pl.pallas_call(
        _merge_kernel,
        out_shape=jax.ShapeDtypeStruct((rows, D_MODEL), F32),
        grid=(rows // tm,),
        in_specs=[row(D_MODEL), row(A_WIDTH), row(A_WIDTH), row(B_HEADS * KV_LORA)] + [full(wts[n]) for n in names],
        out_specs=row(D_MODEL),
        compiler_params=_cparams(("parallel",)),
        name="merge",
    )(x, oa, gate, lat, *[wts[n] for n in names])


N_RANK = PK_TOPK + 1
_CAND_PAIRS = [(r1, r2) for r1 in range(N_RANK) for r2 in range(N_RANK) if (r1 + 1) * (r2 + 1) <= N_RANK]


def _peer_topk_kernel(h_ref, fnw_ref, wpq_ref, sk_ref, xn_ref, thr_ref, cw_ref, s2_ref, a2_ref, s1_scr, top_scr):
    xn = _rms(h_ref[...], fnw_ref[...]).astype(BF16)
    xn_ref[...] = xn
    qt = _dot_nt(wpq_ref[...], xn)
    for p in range(2):
        for h in range(PK_HEADS):
            ph = p * PK_HEADS + h
            st = _dot(sk_ref[ph], qt[ph * PK_DIM:(ph + 1) * PK_DIM].astype(BF16))
            if p == 0:
                s1_scr[h] = st
            else:
                s2_ref[h] = st
            work = st
            for r in range(N_RANK):
                mx = jnp.max(work, axis=0, keepdims=True)
                top_scr[p, r, h:h + 1, :] = mx
                work = jnp.where(work == mx, -jnp.inf, work)
    a_top = [top_scr[0, r] for r in range(N_RANK)]
    b_top = [top_scr[1, r] for r in range(N_RANK)]
    cands = [a_top[r1] + b_top[r2] for (r1, r2) in _CAND_PAIRS]
    best = []
    for r in range(N_RANK):
        mx = functools.reduce(jnp.maximum, cands)
        best.append(mx)
        cands = [jnp.where(cd == mx, -jnp.inf, cd) for cd in cands]
    tau = 0.5 * (best[PK_TOPK - 1] + best[PK_TOPK])
    z = functools.reduce(lambda u, v_: u + v_, [jnp.exp(bk - best[0]) for bk in best[:PK_TOPK]])
    zinv = 1.0 / z
    for h in range(PK_HEADS):
        s1 = s1_scr[h]
        thr_ref[h] = tau[h:h + 1, :] - s1
        cw_ref[h] = jnp.exp(s1 - a_top[0][h:h + 1, :]) * zinv[h:h + 1, :]
        a2_ref[h] = jnp.exp(s2_ref[h] - b_top[0][h:h + 1, :])


def _peer_topk_call(hmid, wts, tt):
    rows = hmid.shape[0]
    full = lambda a: pl.BlockSpec(a.shape, lambda i: (0,) * a.ndim)
    tok = pl.BlockSpec((PK_HEADS, N_KEYS, tt), lambda i: (0, 0, i))
    tshape = jax.ShapeDtypeStruct((PK_HEADS, N_KEYS, rows), F32)
    names = ("fnw", "wpq", "sk")
    return pl.pallas_call(
        _peer_topk_kernel,
        out_shape=[jax.ShapeDtypeStruct((rows, D_MODEL), BF16), tshape, tshape, tshape, tshape],
        grid=(rows // tt,),
        in_specs=[pl.BlockSpec((tt, D_MODEL), lambda i: (i, 0))] + [full(wts[n]) for n in names],
        out_specs=[pl.BlockSpec((tt, D_MODEL), lambda i: (i, 0)), tok, tok, tok, tok],
        scratch_shapes=[pltpu.VMEM((PK_HEADS, N_KEYS, tt), F32), pltpu.VMEM((2, N_RANK, PK_HEADS, tt), F32)],
        compiler_params=_cparams(("parallel",)),
        name="peer_topk",
    )(hmid, *[wts[n] for n in names])


def _peer_dense_kernel(xn_ref, thr_ref, cw_ref, s2_ref, a2_ref, u_ref, vt_ref, hm_ref, fw_ref, y_ref, acc_scr, *, n_i1):
    kstep = pl.program_id(1)

    @pl.when(kstep == 0)
    def _():
        acc_scr[...] = jnp.zeros_like(acc_scr)

    xn = xn_ref[...]
    gs = []
    for j in range(n_i1):
        act = _dot_nt(u_ref[j * N_KEYS:(j + 1) * N_KEYS, :], xn)
        gl = 0.5 * act * (1.0 + lax.erf(act * (1.0 / math.sqrt(2.0))))
        gate = jnp.zeros_like(act)
        for h in range(PK_HEADS):
            sel = jnp.where(s2_ref[h] >= thr_ref[h, j:j + 1, :], a2_ref[h], 0.0)
            gate = gate + sel * cw_ref[h, j:j + 1, :]
        gs.append((gate * gl).astype(BF16))
    acc_scr[...] += _dot(vt_ref[...], jnp.concatenate(gs, axis=0))

    @pl.when(kstep == pl.num_programs(1) - 1)
    def _():
        hfin = hm_ref[...] + jnp.transpose(acc_scr[...])
        y_ref[...] = _rms(hfin, fw_ref[...])


def _peer_dense_call(xn, thr, cw, s2, a2, hmid, wts, tt, n_i1):
    rows = xn.shape[0]
    et = n_i1 * N_KEYS
    n_exp = wts["u"].shape[0]
    tokrow = pl.BlockSpec((tt, D_MODEL), lambda i, k: (i, 0))
    per_i1 = pl.BlockSpec((PK_HEADS, n_i1, tt), lambda i, k: (0, k, i))
    per_i2 = pl.BlockSpec((PK_HEADS, N_KEYS, tt), lambda i, k: (0, 0, i))
    return pl.pallas_call(
        functools.partial(_peer_dense_kernel, n_i1=n_i1),
        out_shape=jax.ShapeDtypeStruct((rows, D_MODEL), F32),
        grid=(rows // tt, n_exp // et),
        in_specs=[tokrow, per_i1, per_i1, per_i2, per_i2,
                  pl.BlockSpec((et, D_MODEL), lambda i, k: (k, 0)),
                  pl.BlockSpec((D_MODEL, et), lambda i, k: (0, k)),
                  tokrow,
                  pl.BlockSpec((1, D_MODEL), lambda i, k: (0, 0))],
        out_specs=tokrow,
        scratch_shapes=[pltpu.VMEM((D_MODEL, tt), F32)],
        compiler_params=_cparams(("parallel", "arbitrary")),
        name="peer_dense",
    )(xn, thr, cw, s2, a2, wts["u"], wts["vt"], hmid, wts["finw"])


def _rope_tables(pos):
    half = D_ROPE // 2
    inv = ROPE_THETA ** (-jnp.arange(half, dtype=F32) * 2.0 / D_ROPE)
    ang = pos.astype(F32)[:, None] * inv[None, :]
    cos = jnp.cos(ang)
    sin = jnp.sin(ang)
    cos = jnp.tile(cos, (1, LANES // half))
    sin = jnp.tile(jnp.concatenate([-sin, sin], axis=-1), (1, LANES // D_ROPE))
    return cos, sin


def _pick_tile(rows, pref):
    t = pref
    while rows % t:
        t //= 2
    return t


def kernel(x_prompt, x_sample, cache_kv_latent, cache_k_rope, state_hgrn, page_table, meta_tokens, attn_norm_w, w_in, lb_logits, q_norm_w, w_uq, kv_norm_w, w_uk, w_uv, a_out_norm_w, b_out_norm_w, w_out, ffn_norm_w, w_peer_q, peer_sub_keys, peer_u, peer_v, final_norm_w):
    depth = attn_norm_w.shape[0]
    assert depth == 1 and x_prompt.shape[0] == 1
    l = 0
    seq = x_prompt.shape[1]
    nb, t_dec = x_sample.shape[:2]
    past_len = page_table.shape[1] * PAGE_SIZE
    n_small = nb * t_dec
    meta_rows = LANES
    n_pad = meta_rows - N_META

    lower = jax.nn.softmax(lb_logits.astype(F32), axis=0)
    lb = jnp.cumsum(lower, axis=0)[l].reshape(1, A_WIDTH)
    perm = ([h * (D_NOPE + D_ROPE) + n for h in range(B_HEADS) for n in range(D_NOPE)]
            + [h * (D_NOPE + D_ROPE) + D_NOPE + r for h in range(B_HEADS) for r in range(D_ROPE)])
    proj_w = {
        "anw": attn_norm_w[l].reshape(1, D_MODEL),
        "win": jnp.pad(w_in[l], ((0, 0), (0, IN_COLS_PAD - IN_COLS))).astype(BF16),
        "lb": lb,
        "qnw": q_norm_w[l].reshape(1, Q_LORA),
        "wuq": w_uq[l][:, jnp.asarray(perm)].astype(BF16),
        "kvnw": kv_norm_w[l].reshape(1, KV_LORA),
        "wuk": jnp.transpose(w_uk[l], (1, 2, 0)).astype(BF16),
    }
    merge_w = {
        "aonw": a_out_norm_w[l].reshape(1, A_HEADS * A_DV),
        "bonw": b_out_norm_w[l].reshape(1, B_HEADS * D_V),
        "wuv": jnp.transpose(w_uv[l], (1, 0, 2)).astype(BF16),
        "wout": w_out[l].astype(BF16),
    }
    topk_w = {
        "fnw": ffn_norm_w[l].reshape(1, D_MODEL),
        "wpq": jnp.transpose(w_peer_q[l]).astype(BF16),
        "sk": peer_sub_keys[l].reshape(2 * PK_HEADS, N_KEYS, PK_DIM).astype(BF16),
    }
    dense_w = {
        "u": peer_u[l].astype(BF16),
        "vt": jnp.transpose(peer_v[l]).astype(BF16),
        "finw": final_norm_w.reshape(1, D_MODEL),
    }

    xp = x_prompt[0]
    xs = jnp.concatenate([x_sample.reshape(n_small, D_MODEL), jnp.zeros((n_pad, D_MODEL), F32),
                          meta_tokens.astype(F32)], axis=0)
    cos_p, sin_p = _rope_tables(N_META + jnp.arange(seq, dtype=jnp.int32))
    pos_s = jnp.concatenate([jnp.tile(past_len + jnp.arange(t_dec, dtype=jnp.int32), nb),
                             jnp.zeros((n_pad,), jnp.int32), jnp.arange(N_META, dtype=jnp.int32)])
    cos_s, sin_s = _rope_tables(pos_s)

    qa_p, ka_p, va_p, g_p, gate_p, c_p, kpe_p, qcat_p, kcat_p = _proj_call(xp, cos_p, sin_p, proj_w, _pick_tile(seq, 256))
    qa_s, ka_s, va_s, g_s, gate_s, c_s, kpe_s, qcat_s, kcat_s = _proj_call(xs, cos_s, sin_s, proj_w, LANES)

    m0 = n_small
    zero_state = jnp.zeros((1, A_HEADS, A_DK, A_DV), F32)
    _, s_meta = _hgrn_call(qa_s[m0:], ka_s[m0:], va_s[m0:], g_s[m0:], zero_state, meta_rows, SUB_CHUNK)
    oa_p, s_prompt = _hgrn_call(qa_p, ka_p, va_p, g_p, s_meta, LANES, SUB_CHUNK)
    oa_s, s_sample = _hgrn_call(qa_s[:m0], ka_s[:m0], va_s[:m0], g_s[:m0], state_hgrn[l].astype(F32), t_dec, t_dec)

    lat_p = _mla_prompt_call(qcat_p, kcat_p, kcat_s[m0:], _pick_tile(seq, 512))
    q3 = qcat_s[:m0].astype(F32).reshape(nb, t_dec, B_HEADS * Q_CAT)
    lat_s = _mla_sample_call(page_table, q3, c_s[:m0].reshape(nb, t_dec, KV_LORA), kpe_s[:m0].reshape(nb, t_dec, D_ROPE),
                             cache_kv_latent[l], cache_k_rope[l])
    lat_s = lat_s.reshape(n_small, B_HEADS * KV_LORA)

    def tail(x, oa, gate, lat):
        rows = x.shape[0]
        hmid = _merge_call(x, oa, gate, lat, merge_w, _pick_tile(rows, 512))
        xn, thr, cw, s2, a2 = _peer_topk_call(hmid, topk_w, _pick_tile(rows, 256))
        return _peer_dense_call(xn, thr, cw, s2, a2, hmid, dense_w, _pick_tile(rows, 512), 8)

    y_p = tail(xp, oa_p, gate_p, lat_p)
    y_s = tail(xs[:m0], oa_s, gate_s[:m0], lat_s)

    mrow = m0 + n_pad
    return (y_p[None],
            y_s.reshape(nb, t_dec, D_MODEL),
            jnp.concatenate([c_s[mrow:], c_p], axis=0)[None, None],
            jnp.concatenate([kpe_s[mrow:], kpe_p], axis=0)[None, None],
            s_prompt[None],
            c_s[:m0].reshape(1, nb, t_dec, KV_LORA),
            kpe_s[:m0].reshape(1, nb, t_dec, D_ROPE),
            s_sample[None])
```
